```python
import math
import jax
import jax.numpy as jnp
from jax import lax
import numpy as np

D_MODEL = 1024
BATCH = 8
SEQ = 2048
DEPTH = 4

EPS = 1e-6
N_EVEN = (DEPTH + 1) // 2
N_ODD = DEPTH // 2

SB_HEAD_DIM = 64
SB_WIDTH = D_MODEL // 2
SB_HEADS = SB_WIDTH // SB_HEAD_DIM
SB_BLOCK = 128
SGU_GROUP_DIM = 64
SGU_WIDTH = D_MODEL // 2
SGU_GROUPS = SGU_WIDTH // SGU_GROUP_DIM
SGU_CHUNK = 128
AB_IN = 3 * SB_WIDTH + 2 * SGU_WIDTH
AB_MIX = SB_WIDTH + SGU_WIDTH
SSD_EXPAND = 2
D_INNER = SSD_EXPAND * D_MODEL
SSD_HEAD_DIM = 64
SSD_HEADS = D_INNER // SSD_HEAD_DIM
SSD_GROUPS = 8
SSD_HEADS_PER_GROUP = SSD_HEADS // SSD_GROUPS
SSD_STATE = 128
SSD_CONV = 4
SSD_CHUNK = 128
SSD_CONV_DIM = D_INNER + 2 * SSD_GROUPS * SSD_STATE
SSD_IN = D_INNER + SSD_CONV_DIM + SSD_HEADS
D_FF = ((8 * D_MODEL // 3 + 127) // 128) * 128
FFN_CONV = 3

kernel_name = "hybrid_stickbreak_sgu_ssd_convffn"


def rms_norm(x, g):
    xf = x.astype(jnp.float32)
    y = xf * lax.rsqrt(jnp.mean(xf * xf, axis=-1, keepdims=True) + EPS)
    return (y * g.astype(jnp.float32)).astype(x.dtype)


def causal_dw_conv(x, w, b):
    k_w, ch = w.shape
    y = lax.conv_general_dilated(
        x, w[:, None, :].astype(x.dtype), window_strides=(1,),
        padding=[(k_w - 1, 0)], dimension_numbers=('NWC', 'WIO', 'NWC'),
        feature_group_count=ch)
    return y + b.astype(x.dtype)


def stick_breaking_attention(q, k, v):
    seq = q.shape[1]
    scale = q.shape[-1] ** -0.5
    outs = []
    for blk in range(seq // SB_BLOCK):
        s0 = blk * SB_BLOCK
        e = s0 + SB_BLOCK
        z = jnp.einsum('bqhd,bkhd->bhqk', q[:, s0:e], k[:, :e]).astype(jnp.float32) * scale
        t_pos = s0 + jnp.arange(SB_BLOCK)[:, None]
        s_pos = jnp.arange(e)[None, :]
        strict = s_pos < t_pos
        log_keep = jnp.where(strict, jax.nn.log_sigmoid(-z), 0.0)
        between = lax.cumsum(log_keep, axis=3, reverse=True) - log_keep
        w = jnp.where(strict, jnp.exp(jax.nn.log_sigmoid(z) + between), 0.0)
        outs.append(jnp.einsum('bhqk,bkhd->bqhd', w.astype(v.dtype), v[:, :e]))
    return jnp.concatenate(outs, axis=1)


def spatial_gating(u, v, w_s, b_s):
    bsz, seq, _ = u.shape
    nc = seq // SGU_CHUNK
    shp = (bsz, nc, SGU_CHUNK, SGU_GROUPS, SGU_GROUP_DIM)
    u = u.reshape(shp)
    vf = v.astype(jnp.float32).reshape(shp)
    mu = jnp.mean(vf, axis=-1, keepdims=True)
    var = jnp.mean(jnp.square(vf - mu), axis=-1, keepdims=True)
    vn = ((vf - mu) * lax.rsqrt(var + EPS)).astype(u.dtype)
    causal = jnp.tril(jnp.ones((SGU_CHUNK, SGU_CHUNK), dtype=bool))
    w = jnp.where(causal, w_s, jnp.zeros_like(w_s))
    mixed = jnp.einsum('gts,bcsgd->bctgd', w, vn) + b_s.T[:, :, None]
    return (u * mixed).reshape(bsz, seq, SGU_WIDTH)


def attn_sgu_mixer(h, w_in, sgu_w, sgu_b, w_out):
    bsz, seq, _ = h.shape
    proj = h @ w_in
    q, k, v, u_g, v_g = jnp.split(
        proj, [SB_WIDTH, 2 * SB_WIDTH, 3 * SB_WIDTH, 3 * SB_WIDTH + SGU_WIDTH], axis=-1)
    heads = lambda t: t.reshape(bsz, seq, SB_HEADS, SB_HEAD_DIM)
    o_a = stick_breaking_attention(heads(q), heads(k), heads(v)).reshape(bsz, seq, SB_WIDTH)
    o_b = spatial_gating(jax.nn.gelu(u_g), jax.nn.gelu(v_g), sgu_w, sgu_b)
    return jnp.concatenate([o_a, o_b], axis=-1) @ w_out


def ssd_scan(x, a, bm, cm):
    bsz, seq = x.shape[:2]
    nc = seq // SSD_CHUNK
    x = x.reshape(bsz, nc, SSD_CHUNK, SSD_GROUPS, SSD_HEADS_PER_GROUP, SSD_HEAD_DIM)
    a = a.reshape(bsz, nc, SSD_CHUNK, SSD_GROUPS, SSD_HEADS_PER_GROUP)
    bm = bm.reshape(bsz, nc, SSD_CHUNK, SSD_GROUPS, SSD_STATE)
    cm = cm.reshape(bsz, nc, SSD_CHUNK, SSD_GROUPS, SSD_STATE)
    a_cum = jnp.cumsum(a, axis=2)
    a_t = jnp.moveaxis(a_cum, 2, -1)
    seg = a_t[..., :, None] - a_t[..., None, :]
    causal = jnp.tril(jnp.ones((SSD_CHUNK, SSD_CHUNK), dtype=bool))
    l_mat = jnp.exp(jnp.where(causal, seg, -jnp.inf))
    cb = jnp.einsum('bclgn,bcsgn->bcgls', cm, bm)
    y_diag = jnp.einsum('bcgrls,bcsgrp->bclgrp', cb[:, :, :, None] * l_mat, x)
    decay = jnp.exp(a_cum[:, :, -1:] - a_cum)
    states = jnp.einsum('bclgn,bclgrp->bcgrpn', bm, x * decay[..., None])
    chunk_decay = jnp.exp(a_cum[:, :, -1])

    def step(hs, inp):
        st, dec = inp
        return hs * dec[..., None, None] + st, hs

    h0 = jnp.zeros((bsz, SSD_GROUPS, SSD_HEADS_PER_GROUP, SSD_HEAD_DIM, SSD_STATE), jnp.float32)
    _, h_prev = lax.scan(step, h0, (jnp.moveaxis(states, 1, 0), jnp.moveaxis(chunk_decay, 1, 0)))
    h_prev = jnp.moveaxis(h_prev, 0, 1)
    y_off = jnp.einsum('bclgn,bcgrpn->bclgrp', cm, h_prev) * jnp.exp(a_cum)[..., None]
    return (y_diag + y_off).reshape(bsz, seq, SSD_HEADS, SSD_HEAD_DIM)


def mamba2_mixer(h, w_in, conv_w, conv_b, dt_bias, a_log, d_skip, norm_g, w_out):
    bsz, seq, _ = h.shape
    proj = h @ w_in
    z, xbc, dt = jnp.split(proj, [D_INNER, D_INNER + SSD_CONV_DIM], axis=-1)
    xbc = jax.nn.silu(causal_dw_conv(xbc, conv_w, conv_b))
    xs, bm, cm = jnp.split(xbc, [D_INNER, D_INNER + SSD_GROUPS * SSD_STATE], axis=-1)
    dt = jax.nn.softplus(dt.astype(jnp.float32) + dt_bias.astype(jnp.float32))
    a_neg = -jnp.exp(a_log.astype(jnp.float32))
    xh = xs.astype(jnp.float32).reshape(bsz, seq, SSD_HEADS, SSD_HEAD_DIM)
    y = ssd_scan(xh * dt[..., None], a_neg * dt,
                 bm.astype(jnp.float32).reshape(bsz, seq, SSD_GROUPS, SSD_STATE),
                 cm.astype(jnp.float32).reshape(bsz, seq, SSD_GROUPS, SSD_STATE))
    y = y + xh * d_skip.astype(jnp.float32)[:, None]
    yg = (y.reshape(bsz, seq, D_INNER) * jax.nn.silu(z.astype(jnp.float32)))
    yg = yg.reshape(bsz, seq, SSD_GROUPS, D_INNER // SSD_GROUPS)
    yg = yg * lax.rsqrt(jnp.mean(yg * yg, axis=-1, keepdims=True) + EPS)
    yg = yg.reshape(bsz, seq, D_INNER) * norm_g.astype(jnp.float32)
    return yg.astype(h.dtype) @ w_out


def conv_ffn(h, w_up, conv_w, conv_b, w_down):
    hu = causal_dw_conv(h @ w_up, conv_w, conv_b)
    g, v = jnp.split(hu, 2, axis=-1)
    return (jax.nn.gelu(g) * v) @ w_down


def setup_inputs(seed: int = 0) -> dict:
    key = jax.random.key(seed)
    ks = jax.random.split(key, 24)
    f32 = jnp.float32
    nrm = lambda k, shape, s: jax.random.normal(k, shape, f32) * s
    gain = lambda k, shape: 1.0 + 0.05 * jax.random.normal(k, shape, f32)
    dt0 = jnp.exp(jax.random.uniform(ks[13], (N_ODD, SSD_HEADS), f32)
                  * (math.log(0.1) - math.log(0.001)) + math.log(0.001))
    return {
        "x": nrm(ks[0], (BATCH, SEQ, D_MODEL), 1.0),
        "mix_pre_g": gain(ks[1], (DEPTH, D_MODEL)),
        "mix_post_g": gain(ks[2], (DEPTH, D_MODEL)),
        "ffn_pre_g": gain(ks[3], (DEPTH, D_MODEL)),
        "ffn_post_g": gain(ks[4], (DEPTH, D_MODEL)),
        "ab_w_in": nrm(ks[5], (N_EVEN, D_MODEL, AB_IN), D_MODEL ** -0.5),
        "sgu_w": nrm(ks[6], (N_EVEN, SGU_GROUPS, SGU_CHUNK, SGU_CHUNK), SGU_CHUNK ** -0.5),
        "sgu_b": 1.0 + nrm(ks[7], (N_EVEN, SGU_GROUPS, SGU_CHUNK), 0.02),
        "ab_w_out": nrm(ks[8], (N_EVEN, AB_MIX, D_MODEL), AB_MIX ** -0.5),
        "ssd_w_in": nrm(ks[9], (N_ODD, D_MODEL, SSD_IN), D_MODEL ** -0.5),
        "ssd_conv_w": nrm(ks[10], (N_ODD, SSD_CONV, SSD_CONV_DIM), SSD_CONV ** -0.5),
        "ssd_conv_b": nrm(ks[11], (N_ODD, SSD_CONV_DIM), 0.02),
        "ssd_dt_bias": dt0 + jnp.log(-jnp.expm1(-dt0)),
        "ssd_a_log": jnp.log(jax.random.uniform(ks[14], (N_ODD, SSD_HEADS), f32, 1.0, 16.0)),
        "ssd_d": gain(ks[15], (N_ODD, SSD_HEADS)),
        "ssd_norm_g": gain(ks[16], (N_ODD, D_INNER)),
        "ssd_w_out": nrm(ks[17], (N_ODD, D_INNER, D_MODEL), D_INNER ** -0.5),
        "ffn_w_up": nrm(ks[18], (DEPTH, D_MODEL, 2 * D_FF), D_MODEL ** -0.5),
        "ffn_conv_w": nrm(ks[19], (DEPTH, FFN_CONV, 2 * D_FF), FFN_CONV ** -0.5),
        "ffn_conv_b": nrm(ks[20], (DEPTH, 2 * D_FF), 0.02),
        "ffn_w_down": nrm(ks[21], (DEPTH, D_FF, D_MODEL), D_FF ** -0.5),
    }


def reference(x, mix_pre_g, mix_post_g, ffn_pre_g, ffn_post_g,
              ab_w_in, sgu_w, sgu_b, ab_w_out,
              ssd_w_in, ssd_conv_w, ssd_conv_b, ssd_dt_bias, ssd_a_log, ssd_d,
              ssd_norm_g, ssd_w_out,
              ffn_w_up, ffn_conv_w, ffn_conv_b, ffn_w_down):
    for i in range(DEPTH):
        j = i // 2
        h = rms_norm(x, mix_pre_g[i])
        if i % 2 == 0:
            m = attn_sgu_mixer(h, ab_w_in[j], sgu_w[j], sgu_b[j], ab_w_out[j])
        else:
            m = mamba2_mixer(h, ssd_w_in[j], ssd_conv_w[j], ssd_conv_b[j], ssd_dt_bias[j],
                             ssd_a_log[j], ssd_d[j], ssd_norm_g[j], ssd_w_out[j])
        x = x + rms_norm(m, mix_post_g[i])
        h = rms_norm(x, ffn_pre_g[i])
        f = conv_ffn(h, ffn_w_up[i], ffn_conv_w[i], ffn_conv_b[i], ffn_w_down[i])
        x = x + rms_norm(f, ffn_post_g[i])
    return x
```

```python
import functools
import math

import jax
import jax.numpy as jnp
from jax import lax
from jax.experimental import pallas as pl
from jax.experimental.pallas import tpu as pltpu

F32 = jnp.float32
BF16 = jnp.bfloat16

EPS = 1e-6
LANES = 128
HEAD_DIM = 64
SGU_CHUNK = 128
SSD_CHUNK = 128
SSD_STATE = 128
SSD_GROUPS = 8
ROW_TILE = 512
ATT_BLOCK = 256
HALO = 8
VMEM_LIMIT = 56 * 1024 * 1024


def _params(semantics):
    return pltpu.CompilerParams(dimension_semantics=semantics,
                                vmem_limit_bytes=VMEM_LIMIT)


def _resident(shape):
    nd = len(shape)
    return pl.BlockSpec(shape, lambda *_: (0,) * nd, pipeline_mode=pl.Buffered(1))


def _rms(x, g):
    ms = jnp.mean(x * x, axis=-1, keepdims=True)
    return x * lax.rsqrt(ms + EPS) * g


def _gelu(x):
    c = math.sqrt(2.0 / math.pi)
    return 0.5 * x * (1.0 + jnp.tanh(c * (x + 0.044715 * (x * x * x))))


def _silu(x):
    return x / (1.0 + jnp.exp(-x))


def _softplus(x):
    return jnp.maximum(x, 0.0) + jnp.log1p(jnp.exp(-jnp.abs(x)))


def _split_dot(a, m, terms):
    out = None
    rem = a
    for _ in range(terms):
        part = rem.astype(BF16)
        rem = rem - part.astype(F32)
        d = jnp.dot(part, m, preferred_element_type=F32)
        out = d if out is None else out + d
    return out


def _ab_in_kernel(x_ref, g_ref, w_ref, sw_ref, sb_ref, avg_ref,
                  q_ref, k_ref, v_ref, ob_ref, *, width):
    tm = x_ref.shape[0]
    h = _rms(x_ref[...], g_ref[...]).astype(BF16)
    proj = jnp.dot(h, w_ref[...], preferred_element_type=F32)
    scale = HEAD_DIM ** -0.5
    q_ref[...] = (proj[:, :width] * scale).astype(BF16)
    k_ref[...] = proj[:, width:2 * width].astype(BF16)
    v_ref[...] = proj[:, 2 * width:3 * width].astype(BF16)
    u = _gelu(proj[:, 3 * width:4 * width])
    vg = _gelu(proj[:, 4 * width:5 * width])
    avg = avg_ref[...]
    mu = _split_dot(vg, avg, 2)
    d = vg - mu
    var = _split_dot(d * d, avg, 2)
    vn = d * lax.rsqrt(var + EPS)

    row = lax.broadcasted_iota(jnp.int32, (SGU_CHUNK, SGU_CHUNK), 0)
    col = lax.broadcasted_iota(jnp.int32, (SGU_CHUNK, SGU_CHUNK), 1)
    causal = col <= row
    low = col < HEAD_DIM
    n_blk = width // LANES
    w_pairs = []
    for cb in range(n_blk):
        w0 = jnp.where(causal, sw_ref[2 * cb], 0.0).astype(BF16)
        w1 = jnp.where(causal, sw_ref[2 * cb + 1], 0.0).astype(BF16)
        w_pairs.append(jnp.concatenate([w0, w1], axis=1))
    for c in range(tm // SGU_CHUNK):
        rs = slice(c * SGU_CHUNK, (c + 1) * SGU_CHUNK)
        for cb in range(n_blk):
            cs = slice(cb * LANES, (cb + 1) * LANES)
            blk = vn[rs, cs]
            rhs = jnp.concatenate([jnp.where(low, blk, 0.0), jnp.where(low, 0.0, blk)],
                                  axis=0).astype(BF16)
            mixed = jnp.dot(w_pairs[cb], rhs, preferred_element_type=F32) + sb_ref[:, cs]
            ob_ref[rs, cs] = (u[rs, cs] * mixed).astype(BF16)


def _ab_in(x, g, w_in, sgu_w, sgu_b_x, avg, width):
    n, d = x.shape
    tm = ROW_TILE
    row_spec = lambda wd: pl.BlockSpec((tm, wd), lambda i: (i, 0))
    out = jax.ShapeDtypeStruct((n, width), BF16)
    return pl.pallas_call(
        functools.partial(_ab_in_kernel, width=width),
        grid=(n // tm,),
        in_specs=[row_spec(d), _resident(g.shape), _resident(w_in.shape),
                  _resident(sgu_w.shape), _resident(sgu_b_x.shape), _resident(avg.shape)],
        out_specs=[row_spec(width)] * 4,
        out_shape=[out] * 4,
        compiler_params=_params(("arbitrary",)),
        name="ab_in",
    )(x, g, w_in, sgu_w, sgu_b_x, avg)


def _sb_attn_kernel(q_ref, kt_ref, v_ref, o_ref):
    n_heads = q_ref.shape[1]
    tq = q_ref.shape[2]
    tk = kt_ref.shape[4]
    qi = pl.program_id(1)
    row = lax.broadcasted_iota(jnp.int32, (tq, tk), 0)
    col = lax.broadcasted_iota(jnp.int32, (tq, tk), 1)
    suffix = (lax.broadcasted_iota(jnp.int32, (tk, tk), 0)
              >= lax.broadcasted_iota(jnp.int32, (tk, tk), 1)).astype(BF16)

    def head_body(h, _):
        qh = q_ref[0, h]

        def kv_body(i, carry):
            acc, run = carry
            j = qi - i
            z = jnp.dot(qh, kt_ref[0, h, j], preferred_element_type=F32)
            strict = (j * tk + col) < (qi * tq + row)
            lk = jnp.minimum(-z, 0.0) - jnp.log1p(jnp.exp(-jnp.abs(z)))
            lk = jnp.where(strict, lk, 0.0)
            cs = _split_dot(lk, suffix, 2)
            w = jnp.where(strict, jnp.exp(z + cs + run), 0.0)
            acc = acc + jnp.dot(w.astype(BF16), v_ref[0, h, j], preferred_element_type=F32)
            return acc, run + cs[:, 0:1]

        acc0 = jnp.zeros((tq, HEAD_DIM), F32)
        run0 = jnp.zeros((tq, 1), F32)
        acc, _ = lax.fori_loop(0, qi + 1, kv_body, (acc0, run0))
        o_ref[0, h] = acc.astype(o_ref.dtype)
        return 0

    lax.fori_loop(0, n_heads, head_body, 0)


def _sb_attention(q, k, v, batch, seq):
    width = q.shape[1]
    heads = width // HEAD_DIM
    tq = tk = ATT_BLOCK
    nkb = seq // tk
    qh = q.reshape(batch, seq, heads, HEAD_DIM).transpose(0, 2, 1, 3)
    kt = k.reshape(batch, nkb, tk, heads, HEAD_DIM).transpose(0, 3, 1, 4, 2)
    vh = v.reshape(batch, nkb, tk, heads, HEAD_DIM).transpose(0, 3, 1, 2, 4)
    o = pl.pallas_call(
        _sb_attn_kernel,
        grid=(batch, seq // tq),
        in_specs=[
            pl.BlockSpec((1, heads, tq, HEAD_DIM), lambda b, i: (b, 0, i, 0)),
            pl.BlockSpec((1, heads, nkb, HEAD_DIM, tk), lambda b, i: (b, 0, 0, 0, 0)),
            pl.BlockSpec((1, heads, nkb, tk, HEAD_DIM), lambda b, i: (b, 0, 0, 0, 0)),
        ],
        out_specs=pl.BlockSpec((1, heads, tq, HEAD_DIM), lambda b, i: (b, 0, i, 0)),
        out_shape=jax.ShapeDtypeStruct((batch, heads, seq, HEAD_DIM), BF16),
        compiler_params=_params(("arbitrary", "arbitrary")),
        name="sb_attn",
    )(qh, kt, vh)
    return o.transpose(0, 2, 1, 3).reshape(batch * seq, width)


def _out_proj_kernel(*refs, n_pairs):
    a_refs = refs[:n_pairs]
    w_refs = refs[n_pairs:2 * n_pairs]
    x_ref, g_ref, o_ref = refs[2 * n_pairs:]
    m = None
    for a_ref, w_ref in zip(a_refs, w_refs):
        d = jnp.dot(a_ref[...], w_ref[...], preferred_element_type=F32)
        m = d if m is None else m + d
    o_ref[...] = x_ref[...] + _rms(m, g_ref[...])


def _out_proj(acts, weights, x, g):
    n, d = x.shape
    tm = ROW_TILE
    n_pairs = len(acts)
    in_specs = ([pl.BlockSpec((tm, a.shape[1]), lambda i: (i, 0)) for a in acts]
                + [_resident(w.shape) for w in weights]
                + [pl.BlockSpec((tm, d), lambda i: (i, 0)), _resident(g.shape)])
    return pl.pallas_call(
        functools.partial(_out_proj_kernel, n_pairs=n_pairs),
        grid=(n // tm,),
        in_specs=in_specs,
        out_specs=pl.BlockSpec((tm, d), lambda i: (i, 0)),
        out_shape=jax.ShapeDtypeStruct((n, d), F32),
        compiler_params=_params(("arbitrary",)),
        name="out_proj",
    )(*acts, *weights, x, g)


def _causal_conv(scr, cols, w, b, tm):
    taps = w.shape[0]
    out = b
    for k in range(taps):
        off = HALO - (taps - 1) + k
        out = out + scr[off:off + tm, cols] * w[k:k + 1]
    return out


def _ffn_kernel(x_ref, gpre_ref, wup_ref, cw_ref, cb_ref, wdn_ref, gpost_ref,
                o_ref, scr, *, d_ff, tn):
    tm = x_ref.shape[0]
    t = pl.program_id(1)

    @pl.when(t == 0)
    def _():
        scr[0:HALO, :] = jnp.zeros((HALO, scr.shape[1]), F32)

    @pl.when(t > 0)
    def _():
        scr[0:HALO, :] = scr[tm:tm + HALO, :]

    x = x_ref[...]
    h = _rms(x, gpre_ref[...]).astype(BF16)
    acc = jnp.zeros((tm, o_ref.shape[1]), F32)
    for c in range(d_ff // tn):
        gcols = slice(c * tn, (c + 1) * tn)
        vcols = slice(d_ff + c * tn, d_ff + (c + 1) * tn)
        for cols in (gcols, vcols):
            scr[HALO:HALO + tm, cols] = jnp.dot(h, wup_ref[:, cols],
                                                preferred_element_type=F32)
        gate = _causal_conv(scr, gcols, cw_ref[:, gcols], cb_ref[:, gcols], tm)
        val = _causal_conv(scr, vcols, cw_ref[:, vcols], cb_ref[:, vcols], tm)
        a = (_gelu(gate) * val).astype(BF16)
        acc = acc + jnp.dot(a, wdn_ref[gcols, :], preferred_element_type=F32)
    o_ref[...] = x + _rms(acc, gpost_ref[...])


def _ffn(x, g_pre, w_up, conv_w, conv_b, w_down, g_post, batch, seq):
    n, d = x.shape
    tm = ROW_TILE
    d_ff = w_down.shape[0]
    tiles = seq // tm
    row = pl.BlockSpec((tm, d), lambda b, t: (b * tiles + t, 0))
    return pl.pallas_call(
        functools.partial(_ffn_kernel, d_ff=d_ff, tn=256),
        grid=(batch, tiles),
        in_specs=[row, _resident(g_pre.shape), _resident(w_up.shape),
                  _resident(conv_w.shape), _resident(conv_b.shape),
                  _resident(w_down.shape), _resident(g_post.shape)],
        out_specs=row,
        out_shape=jax.ShapeDtypeStruct((n, d), F32),
        scratch_shapes=[pltpu.VMEM((tm + HALO, 2 * d_ff), F32)],
        compiler_params=_params(("arbitrary", "arbitrary")),
        name="conv_ffn",
    )(x, g_pre, w_up, conv_w, conv_b, w_down, g_post)


def _ssd_in_kernel(x_ref, g_ref, w_ref, cw_ref, cb_ref, dtb_ref,
                   z_ref, xs_ref, b_ref, c_ref, dt_ref, scr, *, d_inner, tn):
    tm = x_ref.shape[0]
    t = pl.program_id(1)
    conv_dim = scr.shape[1]
    bc_dim = (conv_dim - d_inner) // 2

    @pl.when(t == 0)
    def _():
        scr[0:HALO, :] = jnp.zeros((HALO, conv_dim), F32)

    @pl.when(t > 0)
    def _():
        scr[0:HALO, :] = scr[tm:tm + HALO, :]

    h = _rms(x_ref[...], g_ref[...]).astype(BF16)
    for c in range(d_inner // tn):
        cols = slice(c * tn, (c + 1) * tn)
        z_ref[:, cols] = jnp.dot(h, w_ref[:, cols],
                                 preferred_element_type=F32).astype(z_ref.dtype)
    for c in range(conv_dim // tn):
        cols = slice(c * tn, (c + 1) * tn)
        wcols = slice(d_inner + c * tn, d_inner + (c + 1) * tn)
        scr[HALO:HALO + tm, cols] = jnp.dot(h, w_ref[:, wcols], preferred_element_type=F32)
        y = _silu(_causal_conv(scr, cols, cw_ref[:, cols], cb_ref[:, cols], tm))
        lo = c * tn
        if lo < d_inner:
            xs_ref[:, lo:lo + tn] = y.astype(xs_ref.dtype)
        elif lo < d_inner + bc_dim:
            b_ref[:, lo - d_inner:lo - d_inner + tn] = y.astype(b_ref.dtype)
        else:
            off = lo - d_inner - bc_dim
            c_ref[:, off:off + tn] = y.astype(c_ref.dtype)
    dt_cols = slice(d_inner + conv_dim, d_inner + conv_dim + LANES)
    dt_raw = jnp.dot(h, w_ref[:, dt_cols], preferred_element_type=F32)
    dt_ref[...] = _softplus(dt_raw + dtb_ref[...])


def _ssd_in(x, g, w_in, conv_w, conv_b, dt_bias, d_inner, batch, seq):
    n, d = x.shape
    tm = ROW_TILE
    tiles = seq // tm
    conv_dim = conv_w.shape[1]
    bc_dim = (conv_dim - d_inner) // 2
    idx = lambda b, t: (b * tiles + t, 0)
    row = lambda wd: pl.BlockSpec((tm, wd), idx)
    return pl.pallas_call(
        functools.partial(_ssd_in_kernel, d_inner=d_inner, tn=512),
        grid=(batch, tiles),
        in_specs=[row(d), _resident(g.shape), _resident(w_in.shape),
                  _resident(conv_w.shape), _resident(conv_b.shape),
                  _resident(dt_bias.shape)],
        out_specs=[row(d_inner), row(d_inner), row(bc_dim), row(bc_dim), row(LANES)],
        out_shape=[jax.ShapeDtypeStruct((n, d_inner), BF16),
                   jax.ShapeDtypeStruct((n, d_inner), BF16),
                   jax.ShapeDtypeStruct((n, bc_dim), BF16),
                   jax.ShapeDtypeStruct((n, bc_dim), BF16),
                   jax.ShapeDtypeStruct((n, LANES), F32)],
        scratch_shapes=[pltpu.VMEM((tm + HALO, conv_dim), F32)],
        compiler_params=_params(("arbitrary", "arbitrary")),
        name="ssd_in",
    )(x, g, w_in, conv_w, conv_b, dt_bias)


def _cumsum_rows(a):
    n = a.shape[0]
    row = lax.broadcasted_iota(jnp.int32, a.shape, 0)
    s = 1
    while s < n:
        a = a + jnp.where(row >= s, pltpu.roll(a, s, axis=0), 0.0)
        s *= 2
    return a


def _ssd_scan_kernel(xs_ref, b_ref, c_ref, dt_ref, z_ref, alog_ref, dskip_ref,
                     ng_ref, exp_ref, y_ref, state, *, heads_per_group):
    chunk = xs_ref.shape[0]
    gw = heads_per_group * HEAD_DIM

    @pl.when(pl.program_id(1) == 0)
    def _():
        state[...] = jnp.zeros(state.shape, F32)

    dt = dt_ref[...]
    a = -jnp.exp(alog_ref[...]) * dt
    a_cum = _cumsum_rows(a)
    a_cum_t = a_cum.T
    a_last = a_cum[chunk - 1:chunk, :]
    expand = exp_ref[...]
    dt_x = _split_dot(dt, expand, 3)
    dec_x = _split_dot(dt * jnp.exp(a_last - a_cum), expand, 3)
    eac_x = _split_dot(jnp.exp(a_cum), expand, 3)
    cdec_x = _split_dot(jnp.broadcast_to(jnp.exp(a_last), (HALO, LANES)), expand, 3)

    row = lax.broadcasted_iota(jnp.int32, (chunk, chunk), 0)
    col = lax.broadcasted_iota(jnp.int32, (chunk, chunk), 1)
    causal = col <= row
    low = lax.broadcasted_iota(jnp.int32, (chunk, LANES), 1) < HEAD_DIM

    for g in range(SSD_GROUPS):
        gs = slice(g * gw, (g + 1) * gw)
        ns = slice(g * SSD_STATE, (g + 1) * SSD_STATE)
        xg = xs_ref[:, gs].astype(F32)
        x_dt = xg * dt_x[:, gs]
        x_dec = (xg * dec_x[:, gs]).astype(BF16)
        b_t = b_ref[:, ns].astype(F32).T.astype(BF16)
        c_g = c_ref[:, ns]
        cb = jnp.dot(c_g, b_t, preferred_element_type=F32)
        y_parts = []
        for pb in range(gw // LANES):
            x_blk = x_dt[:, pb * LANES:(pb + 1) * LANES]
            rhs = jnp.concatenate([jnp.where(low, x_blk, 0.0), jnp.where(low, 0.0, x_blk)],
                                  axis=0).astype(BF16)
            lhs = []
            for r in range(LANES // HEAD_DIM):
                hd = g * heads_per_group + pb * (LANES // HEAD_DIM) + r
                seg = a_cum[:, hd:hd + 1] - a_cum_t[hd:hd + 1, :]
                lhs.append((cb * jnp.where(causal, jnp.exp(seg), 0.0)).astype(BF16))
            y_parts.append(jnp.dot(jnp.concatenate(lhs, axis=1), rhs,
                                   preferred_element_type=F32))
        y = jnp.concatenate(y_parts, axis=1)
        h_prev = state[g]
        y = y + jnp.dot(c_g, h_prev.astype(BF16), preferred_element_type=F32) * eac_x[:, gs]
        state[g] = h_prev * cdec_x[0:1, gs] + jnp.dot(b_t, x_dec, preferred_element_type=F32)
        y = y + xg * dskip_ref[:, gs]
        yg = y * _silu(z_ref[:, gs].astype(F32))
        yg = yg * lax.rsqrt(jnp.mean(yg * yg, axis=-1, keepdims=True) + EPS)
        y_ref[:, gs] = (yg * ng_ref[:, gs]).astype(y_ref.dtype)


def _ssd_scan(xs, bm, cm, dt, z, a_log, d_skip_x, norm_g, expand, batch, seq):
    n, d_inner = xs.shape
    chunk = SSD_CHUNK
    nc = seq // chunk
    gw = d_inner // SSD_GROUPS
    idx = lambda b, c: (b * nc + c, 0)
    row = lambda wd: pl.BlockSpec((chunk, wd), idx)
    return pl.pallas_call(
        functools.partial(_ssd_scan_kernel, heads_per_group=gw // HEAD_DIM),
        grid=(batch, nc),
        in_specs=[row(d_inner), row(bm.shape[1]), row(cm.shape[1]), row(LANES), row(d_inner),
                  _resident(a_log.shape), _resident(d_skip_x.shape),
                  _resident(norm_g.shape), _resident(expand.shape)],
        out_specs=row(d_inner),
        out_shape=jax.ShapeDtypeStruct((n, d_inner), BF16),
        scratch_shapes=[pltpu.VMEM((SSD_GROUPS, SSD_STATE, gw), F32)],
        compiler_params=_params(("arbitrary", "arbitrary")),
        name="ssd_scan",
    )(xs, bm, cm, dt, z, a_log, d_skip_x, norm_g, expand)


def _pad_lanes(a, width):
    return jnp.pad(a, [(0, 0)] * (a.ndim - 1) + [(0, width - a.shape[-1])])


def kernel(x, mix_pre_g, mix_post_g, ffn_pre_g, ffn_post_g, ab_w_in, sgu_w, sgu_b, ab_w_out,
           ssd_w_in, ssd_conv_w, ssd_conv_b, ssd_dt_bias, ssd_a_log, ssd_d, ssd_norm_g,
           ssd_w_out, ffn_w_up, ffn_conv_w, ffn_conv_b, ffn_w_down):
    batch, seq, d_model = x.shape
    depth = mix_pre_g.shape[0]
    width = ab_w_out.shape[1] // 2
    d_inner = ssd_w_out.shape[1]
    n_heads = ssd_a_log.shape[1]
    ssd_in = ssd_w_in.shape[2]
    ssd_in_pad = ssd_in - n_heads + LANES

    lane_group = jnp.arange(width) // HEAD_DIM
    avg = (lane_group[:, None] == lane_group[None, :]).astype(BF16) * (1.0 / HEAD_DIM)
    head_of_lane = jnp.arange(d_inner) // HEAD_DIM
    expand = (jnp.arange(LANES)[:, None] == head_of_lane[None, :]).astype(BF16)

    xf = x.reshape(batch * seq, d_model)
    for i in range(depth):
        j = i // 2
        g_pre = mix_pre_g[i][None, :]
        g_post = mix_post_g[i][None, :]
        if i % 2 == 0:
            sgu_b_x = jnp.repeat(sgu_b[j].T, HEAD_DIM, axis=1)
            q, k, v, o_b = _ab_in(xf, g_pre, ab_w_in[j].astype(BF16), sgu_w[j], sgu_b_x,
                                  avg, width)
            o_a = _sb_attention(q, k, v, batch, seq)
            w_out = ab_w_out[j].astype(BF16)
            xf = _out_proj([o_a, o_b], [w_out[:width], w_out[width:]], xf, g_post)
        else:
            w_in = _pad_lanes(ssd_w_in[j], ssd_in_pad).astype(BF16)
            z, xs, bm, cm, dt = _ssd_in(
                xf, g_pre, w_in, ssd_conv_w[j], ssd_conv_b[j][None, :],
                _pad_lanes(ssd_dt_bias[j][None, :], LANES), d_inner, batch, seq)
            y = _ssd_scan(xs, bm, cm, dt, z,
                          _pad_lanes(ssd_a_log[j][None, :], LANES),
                          jnp.repeat(ssd_d[j], HEAD_DIM)[None, :],
                          ssd_norm_g[j][None, :], expand, batch, seq)
            xf = _out_proj([y], [ssd_w_out[j].astype(BF16)], xf, g_post)
        xf = _ffn(xf, ffn_pre_g[i][None, :], ffn_w_up[i].astype(BF16), ffn_conv_w[i],
                  ffn_conv_b[i][None, :], ffn_w_down[i].astype(BF16),
                  ffn_post_g[i][None, :], batch, seq)
    return xf.reshape(batch, seq, d_model)
```
